```python
import math
import jax, jax.numpy as jnp
from jax import lax
import numpy as np

D_MODEL = 4096
BATCH = 4
SEQ = 2048
DEPTH = 1

MIX_WIDTH = D_MODEL
ATTN_WIDTH = MIX_WIDTH // 2
SSM_WIDTH = MIX_WIDTH - ATTN_WIDTH
HEAD_DIM = 128
ATTN_HEADS = ATTN_WIDTH // HEAD_DIM
MOBA_BLOCK = 256
MOBA_TOPK = 3
QUERY_CHUNK = 32
NUM_BUCKETS = 32
MAX_DISTANCE = 128
SSM_GROUP_CH = 16
SSM_GROUPS = SSM_WIDTH // SSM_GROUP_CH
SSM_STATE = 64
IN_PROJ_WIDTH = 3 * ATTN_WIDTH + SSM_WIDTH
PEER_HEADS = 8
PEER_NKEYS = 128
PEER_EXPERTS = PEER_NKEYS * PEER_NKEYS
PEER_TOPK = 16
PEER_QDIM = 256
PEER_TOKEN_CHUNK = 128
PEER_V_SCALE = 0.25
RMS_EPS = 1e-6
NEG_INF = -1e30

kernel_name = 'hymba_moba_s5_peer_block'


def rmsnorm(x, gain):
    xf = x.astype(jnp.float32)
    y = xf * lax.rsqrt(jnp.mean(xf * xf, axis=-1, keepdims=True) + RMS_EPS)
    return (y * gain.astype(jnp.float32)).astype(x.dtype)


def t5_bucket(dist):
    n = jnp.maximum(dist, 0)
    max_exact = NUM_BUCKETS // 2
    ratio = jnp.log(jnp.maximum(n, 1).astype(jnp.float32) / max_exact) / math.log(MAX_DISTANCE / max_exact)
    large = max_exact + (ratio * (NUM_BUCKETS - max_exact)).astype(jnp.int32)
    large = jnp.minimum(large, NUM_BUCKETS - 1)
    return jnp.where(n < max_exact, n, large)


def moba_attention(q, k, v, rel_bias):
    Bsz, S, H, Dh = q.shape
    nb = -(-S // MOBA_BLOCK)
    pad = nb * MOBA_BLOCK - S
    q = q.transpose(0, 2, 1, 3)
    kp = jnp.pad(k.transpose(0, 2, 1, 3), ((0, 0), (0, 0), (0, pad), (0, 0))).reshape(Bsz, H, nb, MOBA_BLOCK, Dh)
    vp = jnp.pad(v.transpose(0, 2, 1, 3), ((0, 0), (0, 0), (0, pad), (0, 0))).reshape(Bsz, H, nb, MOBA_BLOCK, Dh)
    kmean = jnp.mean(kp.astype(jnp.float32), axis=3)
    qblk = jnp.arange(S) // MOBA_BLOCK
    gate = jnp.einsum('bhsd,bhnd->bhsn', q.astype(jnp.float32), kmean)
    past = jnp.arange(nb)[None, :] < qblk[:, None]
    gate = jnp.where(past, gate, NEG_INF)
    ktop = min(MOBA_TOPK, nb)
    _, sel = lax.top_k(gate, ktop)
    valid = sel < qblk[None, None, :, None]
    nq = S // QUERY_CHUNK
    scale = Dh ** -0.5
    hidx = jnp.arange(H)[:, None, None]
    offs = jnp.arange(MOBA_BLOCK)

    def one_chunk(i):
        b = i // nq
        q0 = (i % nq) * QUERY_CHUNK
        qc = lax.dynamic_slice_in_dim(q[b], q0, QUERY_CHUNK, axis=1)
        kb, vb = kp[b], vp[b]
        selc = lax.dynamic_slice_in_dim(sel[b], q0, QUERY_CHUNK, axis=1)
        validc = lax.dynamic_slice_in_dim(valid[b], q0, QUERY_CHUNK, axis=1)
        ksel = kb[hidx, selc]
        vsel = vb[hidx, selc].reshape(H, QUERY_CHUNK, ktop * MOBA_BLOCK, Dh)
        own = q0 // MOBA_BLOCK
        kown = lax.dynamic_index_in_dim(kb, own, axis=1, keepdims=False)
        vown = lax.dynamic_index_in_dim(vb, own, axis=1, keepdims=False)
        s_sel = jnp.einsum('hqd,hqnkd->hqnk', qc, ksel).reshape(H, QUERY_CHUNK, ktop * MOBA_BLOCK)
        s_own = jnp.einsum('hqd,hkd->hqk', qc, kown)
        qpos = q0 + jnp.arange(QUERY_CHUNK)
        kpos_sel = (selc[..., None] * MOBA_BLOCK + offs).reshape(H, QUERY_CHUNK, ktop * MOBA_BLOCK)
        kpos_own = own * MOBA_BLOCK + offs
        kpos = jnp.concatenate([kpos_sel, jnp.broadcast_to(kpos_own, (H, QUERY_CHUNK, MOBA_BLOCK))], axis=-1)
        mask_sel = jnp.broadcast_to(validc[..., None], (H, QUERY_CHUNK, ktop, MOBA_BLOCK)).reshape(H, QUERY_CHUNK, ktop * MOBA_BLOCK)
        mask_own = jnp.broadcast_to(kpos_own[None, None, :] <= qpos[None, :, None], (H, QUERY_CHUNK, MOBA_BLOCK))
        mask = jnp.concatenate([mask_sel, mask_own], axis=-1)
        bias = rel_bias[t5_bucket(qpos[None, :, None] - kpos), hidx]
        logits = jnp.concatenate([s_sel, s_own], axis=-1).astype(jnp.float32) * scale + bias.astype(jnp.float32)
        p = jax.nn.softmax(jnp.where(mask, logits, NEG_INF), axis=-1).astype(v.dtype)
        n_sel = ktop * MOBA_BLOCK
        return (jnp.einsum('hqk,hqkd->hqd', p[..., :n_sel], vsel)
                + jnp.einsum('hqk,hkd->hqd', p[..., n_sel:], vown))

    out = lax.map(one_chunk, jnp.arange(Bsz * nq))
    return out.reshape(Bsz, nq, H, QUERY_CHUNK, Dh).transpose(0, 1, 3, 2, 4).reshape(Bsz, S, H * Dh)


def _complex_affine_combine(e1, e2):
    a1r, a1i, b1r, b1i = e1
    a2r, a2i, b2r, b2i = e2
    return (a2r * a1r - a2i * a1i,
            a2r * a1i + a2i * a1r,
            a2r * b1r - a2i * b1i + b2r,
            a2r * b1i + a2i * b1r + b2i)


def s5_mixer(u, lam_re, lam_im, log_dt, b_re, b_im, c_re, c_im, d_skip, w_glu):
    Bsz, S, _ = u.shape
    ug = u.reshape(Bsz, S, SSM_GROUPS, SSM_GROUP_CH)
    dt = jnp.exp(log_dt)[:, None]
    mag = jnp.exp(lam_re * dt)
    ab_re = mag * jnp.cos(lam_im * dt)
    ab_im = mag * jnp.sin(lam_im * dt)
    den = lam_re * lam_re + lam_im * lam_im
    nr, ni = ab_re - 1.0, ab_im
    coef_re = ((nr * lam_re + ni * lam_im) / den)[..., None]
    coef_im = ((ni * lam_re - nr * lam_im) / den)[..., None]
    bb_re = coef_re * b_re - coef_im * b_im
    bb_im = coef_re * b_im + coef_im * b_re
    bu_re = jnp.einsum('bsgc,gpc->bsgp', ug, bb_re)
    bu_im = jnp.einsum('bsgc,gpc->bsgp', ug, bb_im)
    a_re = jnp.broadcast_to(ab_re.astype(bu_re.dtype), bu_re.shape)
    a_im = jnp.broadcast_to(ab_im.astype(bu_re.dtype), bu_re.shape)
    _, _, h_re, h_im = lax.associative_scan(_complex_affine_combine, (a_re, a_im, bu_re, bu_im), axis=1)
    y = jnp.einsum('gcp,bsgp->bsgc', c_re, h_re) - jnp.einsum('gcp,bsgp->bsgc', c_im, h_im)
    y = y.reshape(Bsz, S, SSM_WIDTH) + d_skip * u
    y = jax.nn.gelu(y, approximate=False)
    return y * jax.nn.sigmoid(y @ w_glu)


def peer_ffn(h, w_q, keys_a, keys_b, u_tab, v_tab):
    Bsz, S, D = h.shape
    T = Bsz * S
    ht = h.reshape(T, D)
    q = (ht @ w_q).reshape(T, PEER_HEADS, PEER_QDIM)
    half = PEER_QDIM // 2
    sa = jnp.einsum('thd,hkd->thk', q[..., :half], keys_a).astype(jnp.float32)
    sb = jnp.einsum('thd,hkd->thk', q[..., half:], keys_b).astype(jnp.float32)
    va, ia = lax.top_k(sa, PEER_TOPK)
    vb, ib = lax.top_k(sb, PEER_TOPK)
    cand = (va[..., :, None] + vb[..., None, :]).reshape(T, PEER_HEADS, PEER_TOPK * PEER_TOPK)
    vbest, cbest = lax.top_k(cand, PEER_TOPK)
    ea = jnp.take_along_axis(ia, cbest // PEER_TOPK, axis=-1)
    eb = jnp.take_along_axis(ib, cbest % PEER_TOPK, axis=-1)
    experts = ea * PEER_NKEYS + eb
    gates = jax.nn.softmax(vbest, axis=-1).astype(h.dtype)
    nchunk = T // PEER_TOKEN_CHUNK

    def one_chunk(args):
        xc, ec, gc = args
        act = jax.nn.gelu(jnp.einsum('td,thkd->thk', xc, u_tab[ec]), approximate=False)
        return jnp.einsum('thk,thkd->td', gc * act, v_tab[ec])

    out = lax.map(one_chunk, (ht.reshape(nchunk, PEER_TOKEN_CHUNK, D),
                              experts.reshape(nchunk, PEER_TOKEN_CHUNK, PEER_HEADS, PEER_TOPK),
                              gates.reshape(nchunk, PEER_TOKEN_CHUNK, PEER_HEADS, PEER_TOPK)))
    return out.reshape(Bsz, S, D)


def setup_inputs(seed: int = 0) -> dict:
    key = jax.random.key(seed)
    ks = jax.random.split(key, 24)

    def nrm(k, shape, scale):
        return jax.random.normal(k, shape, jnp.float32) * scale

    n_idx = jnp.arange(SSM_STATE, dtype=jnp.float32)
    return {
        'x': nrm(ks[0], (BATCH, SEQ, D_MODEL), 1.0),
        'norm_mix_gain': 1.0 + nrm(ks[1], (DEPTH, D_MODEL), 0.02),
        'w_in': nrm(ks[2], (DEPTH, D_MODEL, IN_PROJ_WIDTH), D_MODEL ** -0.5),
        'rel_bias': nrm(ks[3], (NUM_BUCKETS, ATTN_HEADS), 0.5),
        'ssm_lambda_re': -0.5 + nrm(ks[4], (DEPTH, SSM_GROUPS, SSM_STATE), 0.01),
        'ssm_lambda_im': math.pi * n_idx + nrm(ks[5], (DEPTH, SSM_GROUPS, SSM_STATE), 0.01),
        'ssm_log_dt': jax.random.uniform(ks[6], (DEPTH, SSM_GROUPS), jnp.float32, math.log(1e-3), math.log(1e-1)),
        'ssm_b_re': nrm(ks[7], (DEPTH, SSM_GROUPS, SSM_STATE, SSM_GROUP_CH), (2 * SSM_GROUP_CH) ** -0.5),
        'ssm_b_im': nrm(ks[8], (DEPTH, SSM_GROUPS, SSM_STATE, SSM_GROUP_CH), (2 * SSM_GROUP_CH) ** -0.5),
        'ssm_c_re': nrm(ks[9], (DEPTH, SSM_GROUPS, SSM_GROUP_CH, SSM_STATE), (2 * SSM_STATE) ** -0.5),
        'ssm_c_im': nrm(ks[10], (DEPTH, SSM_GROUPS, SSM_GROUP_CH, SSM_STATE), (2 * SSM_STATE) ** -0.5),
        'ssm_d': nrm(ks[11], (DEPTH, SSM_WIDTH), 1.0),
        'ssm_w_glu': nrm(ks[12], (DEPTH, SSM_WIDTH, SSM_WIDTH), SSM_WIDTH ** -0.5),
        'attn_out_gain': 1.0 + nrm(ks[13], (DEPTH, ATTN_WIDTH), 0.02),
        'ssm_out_gain': 1.0 + nrm(ks[14], (DEPTH, SSM_WIDTH), 0.02),
        'w_out': nrm(ks[15], (DEPTH, MIX_WIDTH, D_MODEL), MIX_WIDTH ** -0.5),
        'norm_ffn_gain': 1.0 + nrm(ks[16], (DEPTH, D_MODEL), 0.02),
        'peer_w_q': nrm(ks[17], (DEPTH, D_MODEL, PEER_HEADS * PEER_QDIM), D_MODEL ** -0.5),
        'peer_keys_a': nrm(ks[18], (DEPTH, PEER_HEADS, PEER_NKEYS, PEER_QDIM // 2), (PEER_QDIM // 2) ** -0.5),
        'peer_keys_b': nrm(ks[19], (DEPTH, PEER_HEADS, PEER_NKEYS, PEER_QDIM // 2), (PEER_QDIM // 2) ** -0.5),
        'peer_u': nrm(ks[20], (DEPTH, PEER_EXPERTS, D_MODEL), D_MODEL ** -0.5),
        'peer_v': nrm(ks[21], (DEPTH, PEER_EXPERTS, D_MODEL), PEER_V_SCALE),
        'norm_final_gain': 1.0 + nrm(ks[22], (D_MODEL,), 0.02),
    }


def reference(x, norm_mix_gain, w_in, rel_bias, ssm_lambda_re, ssm_lambda_im, ssm_log_dt,
              ssm_b_re, ssm_b_im, ssm_c_re, ssm_c_im, ssm_d, ssm_w_glu, attn_out_gain,
              ssm_out_gain, w_out, norm_ffn_gain, peer_w_q, peer_keys_a, peer_keys_b,
              peer_u, peer_v, norm_final_gain):
    Bsz, S, _ = x.shape
    for l in range(DEPTH):
        h = rmsnorm(x, norm_mix_gain[l])
        proj = h @ w_in[l]
        q, k, v, u = jnp.split(proj, [ATTN_WIDTH, 2 * ATTN_WIDTH, 3 * ATTN_WIDTH], axis=-1)
        q = q.reshape(Bsz, S, ATTN_HEADS, HEAD_DIM)
        k = k.reshape(Bsz, S, ATTN_HEADS, HEAD_DIM)
        v = v.reshape(Bsz, S, ATTN_HEADS, HEAD_DIM)
        y_attn = moba_attention(q, k, v, rel_bias)
        y_ssm = s5_mixer(u, ssm_lambda_re[l], ssm_lambda_im[l], ssm_log_dt[l], ssm_b_re[l], ssm_b_im[l],
                         ssm_c_re[l], ssm_c_im[l], ssm_d[l], ssm_w_glu[l])
        y = jnp.concatenate([rmsnorm(y_attn, attn_out_gain[l]), rmsnorm(y_ssm, ssm_out_gain[l])], axis=-1)
        x = x + y @ w_out[l]
        x = x + peer_ffn(rmsnorm(x, norm_ffn_gain[l]), peer_w_q[l], peer_keys_a[l], peer_keys_b[l],
                         peer_u[l], peer_v[l])
    return rmsnorm(x, norm_final_gain)
```

```python
import functools
import math

import jax
import jax.numpy as jnp
import numpy as np
from jax import lax
from jax.experimental import pallas as pl
from jax.experimental.pallas import tpu as pltpu

F32 = jnp.float32
BF16 = jnp.bfloat16

HEAD_DIM = 128
MOBA_BLOCK = 256
MOBA_TOPK = 3
NUM_BUCKETS = 32
MAX_DISTANCE = 128
SSM_GROUP_CH = 16
SSM_STATE = 64
SSM_CHUNK = 16
PEER_HEADS = 8
PEER_NKEYS = 128
PEER_TOPK = 16
PEER_QDIM = 256
RMS_EPS = 1e-6
NEG_INF = -1e30

V7X_VMEM_LIMIT_BYTES = 60000 * 1024


def _cparams(*sem):
    return pltpu.CompilerParams(dimension_semantics=sem, vmem_limit_bytes=V7X_VMEM_LIMIT_BYTES)


def _gelu(x):
    return 0.5 * x * (1.0 + lax.erf(x * (1.0 / math.sqrt(2.0))))


def _rms_body(x_ref, g_ref, o_ref):
    x = x_ref[...].astype(F32)
    y = x * lax.rsqrt(jnp.mean(x * x, axis=-1, keepdims=True) + RMS_EPS)
    o_ref[...] = (y * g_ref[...]).astype(o_ref.dtype)


def _rmsnorm(x, gain, out_dtype, tm=256):
    m, d = x.shape
    return pl.pallas_call(
        _rms_body,
        grid=(m // tm,),
        in_specs=[pl.BlockSpec((tm, d), lambda i: (i, 0)), pl.BlockSpec((1, d), lambda i: (0, 0))],
        out_specs=pl.BlockSpec((tm, d), lambda i: (i, 0)),
        out_shape=jax.ShapeDtypeStruct((m, d), out_dtype),
        compiler_params=_cparams("parallel"),
        name="rmsnorm",
    )(x, gain.reshape(1, d).astype(F32))


def _rms2_body(a_ref, b_ref, ga_ref, gb_ref, o_ref):
    wa = a_ref.shape[1]
    for ref, g_ref, lo in ((a_ref, ga_ref, 0), (b_ref, gb_ref, wa)):
        x = ref[...].astype(F32)
        y = x * lax.rsqrt(jnp.mean(x * x, axis=-1, keepdims=True) + RMS_EPS)
        o_ref[:, lo:lo + x.shape[1]] = (y * g_ref[...]).astype(o_ref.dtype)


def _rmsnorm_concat(a, b, gain_a, gain_b, tm=256):
    m, wa = a.shape
    wb = b.shape[1]
    return pl.pallas_call(
        _rms2_body,
        grid=(m // tm,),
        in_specs=[pl.BlockSpec((tm, wa), lambda i: (i, 0)), pl.BlockSpec((tm, wb), lambda i: (i, 0)),
                  pl.BlockSpec((1, wa), lambda i: (0, 0)), pl.BlockSpec((1, wb), lambda i: (0, 0))],
        out_specs=pl.BlockSpec((tm, wa + wb), lambda i: (i, 0)),
        out_shape=jax.ShapeDtypeStruct((m, wa + wb), BF16),
        compiler_params=_cparams("parallel"),
        name="rmsnorm_concat",
    )(a, b, gain_a.reshape(1, wa).astype(F32), gain_b.reshape(1, wb).astype(F32))


def _add_rms_body(x_ref, y_ref, g_ref, o_ref):
    x = x_ref[...] + y_ref[...]
    y = x * lax.rsqrt(jnp.mean(x * x, axis=-1, keepdims=True) + RMS_EPS)
    o_ref[...] = y * g_ref[...]


def _add_rmsnorm(x, y, gain, tm=256):
    m, d = x.shape
    return pl.pallas_call(
        _add_rms_body,
        grid=(m // tm,),
        in_specs=[pl.BlockSpec((tm, d), lambda i: (i, 0)), pl.BlockSpec((tm, d), lambda i: (i, 0)),
                  pl.BlockSpec((1, d), lambda i: (0, 0))],
        out_specs=pl.BlockSpec((tm, d), lambda i: (i, 0)),
        out_shape=jax.ShapeDtypeStruct((m, d), F32),
        compiler_params=_cparams("parallel"),
        name="add_rmsnorm",
    )(x, y, gain.reshape(1, d).astype(F32))


def _mm_body(a_ref, b_ref, o_ref):
    o_ref[...] = jnp.dot(a_ref[...], b_ref[...], preferred_element_type=F32).astype(o_ref.dtype)


def _mm_residual_body(a_ref, b_ref, r_ref, o_ref):
    o_ref[...] = r_ref[...] + jnp.dot(a_ref[...], b_ref[...], preferred_element_type=F32)


def _mm_glu_body(a_ref, b_ref, y_ref, o_ref):
    z = jnp.dot(a_ref[...], b_ref[...], preferred_element_type=F32)
    o_ref[...] = y_ref[...] * jax.nn.sigmoid(z)


def _matmul(a, b, out_dtype, extra=None, body=_mm_body, tm=1024, tn=512, name="matmul"):
    m, k = a.shape
    n = b.shape[1]
    tm, tn = min(tm, m), min(tn, n)
    in_specs = [pl.BlockSpec((tm, k), lambda i, j: (i, 0)), pl.BlockSpec((k, tn), lambda i, j: (0, j))]
    args = [a, b]
    if extra is not None:
        in_specs.append(pl.BlockSpec((tm, tn), lambda i, j: (i, j)))
        args.append(extra)
    return pl.pallas_call(
        body,
        grid=(m // tm, n // tn),
        in_specs=in_specs,
        out_specs=pl.BlockSpec((tm, tn), lambda i, j: (i, j)),
        out_shape=jax.ShapeDtypeStruct((m, n), out_dtype),
        compiler_params=_cparams("parallel", "parallel"),
        name=name,
    )(*args)


def _t5_bucket_table(max_dist):
    n = np.arange(max_dist)
    max_exact = NUM_BUCKETS // 2
    ratio = (np.log(np.maximum(n, 1).astype(np.float32) / np.float32(max_exact))
             / np.float32(math.log(MAX_DISTANCE / max_exact))).astype(np.float32)
    large = max_exact + (ratio * np.float32(NUM_BUCKETS - max_exact)).astype(np.int32)
    large = np.minimum(large, NUM_BUCKETS - 1)
    return np.where(n < max_exact, n, large).astype(np.int32)


def _moba_bias_plan(seq):
    blk = MOBA_BLOCK
    nb = seq // blk
    buckets = _t5_bucket_table(seq)
    far_bucket = int(buckets[-1])
    n_near = nb
    for delta in range(nb):
        lo = max(delta * blk - (blk - 1), 0)
        if np.all(buckets[lo:] == far_bucket):
            n_near = delta
            break
    qi = np.arange(blk)[:, None]
    kj = np.arange(blk)[None, :]
    tiles = [buckets[np.clip(d * blk + qi - kj, 0, seq - 1)] for d in range(max(n_near, 1))]
    return np.stack(tiles).astype(np.int32), n_near, far_bucket


def _moba_body(relb_ref, bkt_ref, q_ref, k_ref, v_ref, o_ref, bias_sc, *, nb, n_near, far_bucket):
    blk = MOBA_BLOCK
    h = pl.program_id(0)
    b = pl.program_id(1)

    @pl.when(b == 0)
    def _():
        for d in range(bias_sc.shape[0]):
            bk = bkt_ref[d]
            acc = jnp.zeros((blk, blk), F32)
            for j in range(NUM_BUCKETS):
                acc = jnp.where(bk == j, relb_ref[j, h], acc)
            bias_sc[d] = acc

    far_bias = relb_ref[far_bucket, h]
    scale = HEAD_DIM ** -0.5
    q = q_ref[...]
    k = k_ref[...]
    v = v_ref[...]

    kmean = jnp.concatenate(
        [jnp.mean(k[n * blk:(n + 1) * blk].astype(F32), axis=0, keepdims=True) for n in range(nb)], axis=0)
    km_hi = kmean.astype(BF16)
    km_lo = (kmean - km_hi.astype(F32)).astype(BF16)
    dn = (((1,), (1,)), ((), ()))
    gate = (lax.dot_general(q, km_hi, dn, preferred_element_type=F32)
            + lax.dot_general(q, km_lo, dn, preferred_element_type=F32))

    lane = lax.broadcasted_iota(jnp.int32, (blk, nb), 1)
    row_i = lax.broadcasted_iota(jnp.int32, (blk, blk), 0)
    col_i = lax.broadcasted_iota(jnp.int32, (blk, blk), 1)
    causal = col_i <= row_i

    for qb in range(nb):
        qs = q[qb * blk:(qb + 1) * blk]
        g = gate[qb * blk:(qb + 1) * blk]
        tiles = []
        for kb in range(qb + 1):
            s = lax.dot_general(qs, k[kb * blk:(kb + 1) * blk], dn, preferred_element_type=F32)
            delta = qb - kb
            bias = bias_sc[delta] if delta < n_near else far_bias
            s = s * scale + bias
            if kb == qb:
                mask = causal
            else:
                gn = g[:, kb:kb + 1]
                ahead = ((g > gn) | ((g == gn) & (lane < kb))) & (lane < qb)
                rank = jnp.sum(ahead.astype(F32), axis=1, keepdims=True)
                mask = rank < float(MOBA_TOPK)
            tiles.append(jnp.where(mask, s, NEG_INF))
        m = tiles[0]
        for t in tiles[1:]:
            m = jnp.maximum(m, t)
        m = jnp.max(m, axis=1, keepdims=True)
        acc = jnp.zeros((blk, HEAD_DIM), F32)
        den = jnp.zeros((blk, 1), F32)
        for kb, t in enumerate(tiles):
            p = jnp.exp(t - m)
            den = den + jnp.sum(p, axis=1, keepdims=True)
            acc = acc + jnp.dot(p.astype(BF16), v[kb * blk:(kb + 1) * blk], preferred_element_type=F32)
        o_ref[qb * blk:(qb + 1) * blk, :] = (acc / den).astype(o_ref.dtype)


def _moba_attention(qkv, rel_bias, batch, seq, heads):
    nb = seq // MOBA_BLOCK
    tiles, n_near, far_bucket = _moba_bias_plan(seq)
    body = functools.partial(_moba_body, nb=nb, n_near=n_near, far_bucket=far_bucket)
    return pl.pallas_call(
        body,
        grid=(heads, batch),
        in_specs=[pl.BlockSpec(memory_space=pltpu.SMEM),
                  pl.BlockSpec(tiles.shape, lambda h, b: (0, 0, 0)),
                  pl.BlockSpec((seq, HEAD_DIM), lambda h, b: (b, h)),
                  pl.BlockSpec((seq, HEAD_DIM), lambda h, b: (b, heads + h)),
                  pl.BlockSpec((seq, HEAD_DIM), lambda h, b: (b, 2 * heads + h))],
        out_specs=pl.BlockSpec((seq, HEAD_DIM), lambda h, b: (b, h)),
        out_shape=jax.ShapeDtypeStruct((batch * seq, heads * HEAD_DIM), F32),
        scratch_shapes=[pltpu.VMEM(tiles.shape, F32)],
        compiler_params=_cparams("arbitrary", "arbitrary"),
        name="moba_attention",
    )(rel_bias.astype(F32), jnp.asarray(tiles), qkv, qkv, qkv)


def _ssm_prep(lam_re, lam_im, log_dt, b_re, b_im, c_re, c_im, d_skip, nsteps):
    hp = lax.Precision.HIGHEST
    L, C = SSM_CHUNK, SSM_GROUP_CH
    G, P = lam_re.shape
    dt = jnp.exp(log_dt)[:, None]

    def apow(nvec):
        nv = jnp.asarray(nvec, F32)[None, :, None]
        mag = jnp.exp(lam_re[:, None, :] * dt[:, None, :] * nv)
        ang = lam_im[:, None, :] * dt[:, None, :] * nv
        return mag * jnp.cos(ang), mag * jnp.sin(ang)

    ab_re, ab_im = (t[:, 0] for t in apow([1.0]))
    den = lam_re * lam_re + lam_im * lam_im
    nr, ni = ab_re - 1.0, ab_im
    coef_re = ((nr * lam_re + ni * lam_im) / den)[..., None]
    coef_im = ((ni * lam_re - nr * lam_im) / den)[..., None]
    bb_re = coef_re * b_re - coef_im * b_im
    bb_im = coef_re * b_im + coef_im * b_re

    ar, ai = apow(np.arange(L + 1))
    car = c_re[:, None] * ar[:, :, None, :] - c_im[:, None] * ai[:, :, None, :]
    cai = c_re[:, None] * ai[:, :, None, :] + c_im[:, None] * ar[:, :, None, :]
    kern = (jnp.einsum('gtcp,gpd->gtcd', car[:, :L], bb_re, precision=hp)
            - jnp.einsum('gtcp,gpd->gtcd', cai[:, :L], bb_im, precision=hp))
    tt = np.arange(L)[:, None]
    ss = np.arange(L)[None, :]
    kts = kern[:, np.clip(tt - ss, 0, L - 1)]
    kts = jnp.where(jnp.asarray(tt >= ss)[None, :, :, None, None], kts, 0.0)
    mt = kts.transpose(0, 2, 4, 1, 3).reshape(G, L * C, L * C)

    rev_r, rev_i = ar[:, L - 1::-1][:, :L], ai[:, L - 1::-1][:, :L]
    w_r = rev_r[..., None] * bb_re[:, None] - rev_i[..., None] * bb_im[:, None]
    w_i = rev_r[..., None] * bb_im[:, None] + rev_i[..., None] * bb_re[:, None]
    w_r = w_r.transpose(0, 1, 3, 2).reshape(G, L * C, P)
    w_i = w_i.transpose(0, 1, 3, 2).reshape(G, L * C, P)

    v_r = car[:, 1:].transpose(0, 3, 1, 2).reshape(G, P, L * C)
    v_i = (-cai[:, 1:]).transpose(0, 3, 1, 2).reshape(G, P, L * C)

    apr, api = apow([float(L * (1 << k)) for k in range(nsteps)])
    ap = jnp.stack([apr, api], axis=2)
    d_flat = jnp.tile(d_skip.reshape(G, 1, C), (1, L, 1)).reshape(G, 1, L * C)
    return (mt.astype(BF16), w_r.astype(BF16), w_i.astype(BF16), v_r.astype(BF16), v_i.astype(BF16),
            ap.astype(F32), d_flat.astype(F32))


def _ssm_body(u_ref, mt_ref, wr_ref, wi_ref, vr_ref, vi_ref, ap_ref, d_ref, o_ref, *, nsteps, cps):
    gb, rows, _ = u_ref.shape
    pdim = wr_ref.shape[2]
    chunk = lax.broadcasted_iota(jnp.int32, (rows, pdim), 0) % cps
    for gi in range(gb):
        u = u_ref[gi]
        ub = u.astype(BF16)
        y = jnp.dot(ub, mt_ref[gi], preferred_element_type=F32)
        hr = jnp.dot(ub, wr_ref[gi], preferred_element_type=F32)
        hi = jnp.dot(ub, wi_ref[gi], preferred_element_type=F32)
        for k in range(nsteps):
            sh = 1 << k
            ar = ap_ref[gi, k, 0:1, :]
            ai = ap_ref[gi, k, 1:2, :]
            keep = chunk >= sh
            sr = jnp.where(keep, pltpu.roll(hr, sh, axis=0), 0.0)
            si = jnp.where(keep, pltpu.roll(hi, sh, axis=0), 0.0)
            hr, hi = hr + (ar * sr - ai * si), hi + (ar * si + ai * sr)
        keep = chunk >= 1
        pr = jnp.where(keep, pltpu.roll(hr, 1, axis=0), 0.0)
        pi = jnp.where(keep, pltpu.roll(hi, 1, axis=0), 0.0)
        y = y + jnp.dot(pr.astype(BF16), vr_ref[gi], preferred_element_type=F32)
        y = y + jnp.dot(pi.astype(BF16), vi_ref[gi], preferred_element_type=F32)
        y = y + d_ref[gi] * u
        o_ref[gi] = _gelu(y)


def _s5_mixer(u, batch, seq, lam_re, lam_im, log_dt, b_re, b_im, c_re, c_im, d_skip, gb=4):
    L, C = SSM_CHUNK, SSM_GROUP_CH
    G, P = lam_re.shape
    cps = seq // L
    rows = batch * cps
    nsteps = max(1, math.ceil(math.log2(cps)))
    tabs = _ssm_prep(lam_re, lam_im, log_dt, b_re, b_im, c_re, c_im, d_skip, nsteps)
    uf = u.reshape(batch, cps, L, G, C).transpose(3, 0, 1, 2, 4).reshape(G, rows, L * C)
    lc = L * C
    body = functools.partial(_ssm_body, nsteps=nsteps, cps=cps)
    g3 = lambda g: (g, 0, 0)
    yf = pl.pallas_call(
        body,
        grid=(G // gb,),
        in_specs=[pl.BlockSpec((gb, rows, lc), g3), pl.BlockSpec((gb, lc, lc), g3),
                  pl.BlockSpec((gb, lc, P), g3), pl.BlockSpec((gb, lc, P), g3),
                  pl.BlockSpec((gb, P, lc), g3), pl.BlockSpec((gb, P, lc), g3),
                  pl.BlockSpec((gb, nsteps, 2, P), lambda g: (g, 0, 0, 0)),
                  pl.BlockSpec((gb, 1, lc), g3)],
        out_specs=pl.BlockSpec((gb, rows, lc), g3),
        out_shape=jax.ShapeDtypeStruct((G, rows, lc), F32),
        compiler_params=_cparams("parallel"),
        name="s5_chunked",
    )(uf, *tabs)
    return yf.reshape(G, batch, cps, L, C).transpose(1, 2, 3, 0, 4).reshape(batch * seq, G * C)


def _split_bf16(x):
    hi = x.astype(BF16)
    return hi, (x - hi.astype(F32)).astype(BF16)


def _top_values(s, count):
    rows = lax.broadcasted_iota(jnp.int32, s.shape, 0)
    out = []
    for _ in range(count):
        m = jnp.max(s, axis=0, keepdims=True)
        first = jnp.min(jnp.where(s == m, rows, s.shape[0]), axis=0, keepdims=True)
        out.append(m)
        s = jnp.where(rows == first, -jnp.inf, s)
    return out


def _peer_topk_body(q_ref, ka_ref, kb_ref, sa_ref, sb_ref, ea_ref, eb_ref, tau_ref):
    half = PEER_QDIM // 2
    dn = (((1,), (1,)), ((), ()))
    for h in range(PEER_HEADS):
        scores = []
        for key_ref, lo in ((ka_ref, h * PEER_QDIM), (kb_ref, h * PEER_QDIM + half)):
            qh, ql = _split_bf16(q_ref[:, lo:lo + half])
            kh, kl = _split_bf16(key_ref[h])
            scores.append(lax.dot_general(kh, qh, dn, preferred_element_type=F32)
                          + lax.dot_general(kh, ql, dn, preferred_element_type=F32)
                          + lax.dot_general(kl, qh, dn, preferred_element_type=F32))
        sa, sb = scores
        va = _top_values(sa, PEER_TOPK)
        vb = _top_values(sb, PEER_TOPK)
        pairs = [va[i] + vb[j] for i in range(PEER_TOPK) for j in range(PEER_TOPK)
                 if (i + 1) * (j + 1) <= PEER_TOPK]
        pad = (-len(pairs)) % 8
        cand = jnp.concatenate(pairs + [jnp.full_like(pairs[0], -jnp.inf)] * pad, axis=0)
        best = _top_values(cand, PEER_TOPK)
        z = best[0] * 0.0
        for vk in best:
            z = z + jnp.exp(vk - best[0])
        sa_ref[h] = sa
        sb_ref[h] = sb
        ea_ref[h] = jnp.exp(sa - va[0]) / z
        eb_ref[h] = jnp.exp(sb - vb[0])
        tau_ref[h] = best[-1]


def _peer_topk(q2, keys_a, keys_b, tt=256):
    t = q2.shape[0]
    hh, nk, half = keys_a.shape
    big = jax.ShapeDtypeStruct((hh, nk, t), F32)
    bspec = pl.BlockSpec((hh, nk, tt), lambda i: (0, 0, i))
    return pl.pallas_call(
        _peer_topk_body,
        grid=(t // tt,),
        in_specs=[pl.BlockSpec((tt, q2.shape[1]), lambda i: (i, 0)),
                  pl.BlockSpec((hh, nk, half), lambda i: (0, 0, 0)),
                  pl.BlockSpec((hh, nk, half), lambda i: (0, 0, 0))],
        out_specs=[bspec, bspec, bspec, bspec, pl.BlockSpec((hh, 1, tt), lambda i: (0, 0, i))],
        out_shape=[big, big, big, big, jax.ShapeDtypeStruct((hh, 1, t), F32)],
        compiler_params=_cparams("parallel"),
        name="peer_topk",
    )(q2, keys_a.astype(F32), keys_b.astype(F32))


def _peer_dense_body(x_ref, u_ref, v_ref, sa_ref, sb_ref, ea_ref, eb_ref, tau_ref, o_ref):
    e = pl.program_id(1)
    nk = sb_ref.shape[1]
    jb = u_ref.shape[0] // nk

    @pl.when(e == 0)
    def _():
        o_ref[...] = jnp.zeros_like(o_ref)

    act = lax.dot_general(u_ref[...], x_ref[...], (((1,), (1,)), ((), ())), preferred_element_type=F32)
    hid = []
    for jj in range(jb):
        sa_j = sa_ref[e * jb + jj]
        ea_j = ea_ref[e * jb + jj]
        gate = jnp.zeros((nk, x_ref.shape[0]), F32)
        for h in range(PEER_HEADS):
            s = sa_j[h:h + 1] + sb_ref[h]
            gate = gate + jnp.where(s >= tau_ref[h], ea_j[h:h + 1] * eb_ref[h], 0.0)
        hid.append((_gelu(act[jj * nk:(jj + 1) * nk]) * gate).astype(BF16))
    hid = jnp.concatenate(hid, axis=0)
    o_ref[...] += lax.dot_general(hid, v_ref[...], (((0,), (0,)), ((), ())), preferred_element_type=F32)


def _peer_dense(xn, u_tab, v_tab, sa, sb, ea, eb, tau, tt=512, eblk=512):
    t, d = xn.shape
    e = u_tab.shape[0]
    hh, nk, _ = sa.shape
    tt, eblk = min(tt, t), min(eblk, e)
    once = pl.Buffered(1)
    tok3 = lambda i, j: (0, 0, i)
    sa = sa.transpose(1, 0, 2)
    ea = ea.transpose(1, 0, 2)
    return pl.pallas_call(
        _peer_dense_body,
        grid=(t // tt, e // eblk),
        in_specs=[pl.BlockSpec((tt, d), lambda i, j: (i, 0), pipeline_mode=once),
                  pl.BlockSpec((eblk, d), lambda i, j: (j, 0)),
                  pl.BlockSpec((eblk, d), lambda i, j: (j, 0)),
                  pl.BlockSpec((nk, hh, tt), tok3, pipeline_mode=once),
                  pl.BlockSpec((hh, nk, tt), tok3, pipeline_mode=once),
                  pl.BlockSpec((nk, hh, tt), tok3, pipeline_mode=once),
                  pl.BlockSpec((hh, nk, tt), tok3, pipeline_mode=once),
                  pl.BlockSpec((hh, 1, tt), tok3, pipeline_mode=once)],
        out_specs=pl.BlockSpec((tt, d), lambda i, j: (i, 0)),
        out_shape=jax.ShapeDtypeStruct((t, d), F32),
        compiler_params=_cparams("parallel", "arbitrary"),
        name="peer_dense",
    )(xn, u_tab, v_tab, sa, sb, ea, eb, tau)


def kernel(x, norm_mix_gain, w_in, rel_bias, ssm_lambda_re, ssm_lambda_im, ssm_log_dt, ssm_b_re, ssm_b_im, ssm_c_re, ssm_c_im, ssm_d, ssm_w_glu, attn_out_gain, ssm_out_gain, w_out, norm_ffn_gain, peer_w_q, peer_keys_a, peer_keys_b, peer_u, peer_v, norm_final_gain):
    batch, seq, d_model = x.shape
    depth = w_in.shape[0]
    attn_width = attn_out_gain.shape[1]
    heads = attn_width // HEAD_DIM
    xt = x.reshape(batch * seq, d_model)
    for l in range(depth):
        h = _rmsnorm(xt, norm_mix_gain[l], BF16)
        w_in_l = w_in[l].astype(BF16)
        qkv = _matmul(h, w_in_l[:, :3 * attn_width], BF16, name="in_proj_qkv")
        u = _matmul(h, w_in_l[:, 3 * attn_width:], F32, name="in_proj_u")
        y_attn = _moba_attention(qkv, rel_bias, batch, seq, heads)
        y_pre = _s5_mixer(u, batch, seq, ssm_lambda_re[l], ssm_lambda_im[l], ssm_log_dt[l], ssm_b_re[l],
                          ssm_b_im[l], ssm_c_re[l], ssm_c_im[l], ssm_d[l])
        y_ssm = _matmul(y_pre.astype(BF16), ssm_w_glu[l].astype(BF16), F32, extra=y_pre, body=_mm_glu_body,
                        name="ssm_glu")
        y = _rmsnorm_concat(y_attn, y_ssm, attn_out_gain[l], ssm_out_gain[l])
        xt = _matmul(y, w_out[l].astype(BF16), F32, extra=xt, body=_mm_residual_body, name="out_proj")
        hn = _rmsnorm(xt, norm_ffn_gain[l], BF16)
        q2 = _matmul(hn, peer_w_q[l].astype(BF16), F32, name="peer_query")
        sa, sb, ea, eb, tau = _peer_topk(q2, peer_keys_a[l], peer_keys_b[l])
        peer_out = _peer_dense(hn, peer_u[l].astype(BF16), peer_v[l].astype(BF16), sa, sb, ea, eb, tau)
        if l + 1 < depth:
            xt = xt + peer_out
    out = _add_rmsnorm(xt, peer_out, norm_final_gain)
    return out.reshape(batch, seq, d_model)
```
